```python
import math
import jax, jax.numpy as jnp
from jax import lax
import numpy as np

D_MODEL = 2048
BATCH = 32
SEQ = 256
DEPTH = 2
DEC_BATCH = 4
DEC_SEQ = 1024
PAST_LEN = 512

GRID_W = 64
N_HEADS = 8
NOPE_DIM = 128
ROPE_DIM = 64
QK_DIM = NOPE_DIM + ROPE_DIM
V_DIM = 128
Q_LORA = D_MODEL // 4
KV_LORA = D_MODEL // 8
ATTN_WIDTH = N_HEADS * V_DIM
POOL_WIDTH = D_MODEL - ATTN_WIDTH
POOL_WINDOWS = (2, 4, 8, 16)
N_POOL_GROUPS = len(POOL_WINDOWS)
POOL_GROUP_DIM = POOL_WIDTH // N_POOL_GROUPS
IN_COLS = Q_LORA + KV_LORA + ROPE_DIM + POOL_WIDTH
N_EXPERT_GROUPS = 4
EXPERTS_PER_GROUP = 8
N_EXPERTS = N_EXPERT_GROUPS * EXPERTS_PER_GROUP
TOP_K = 2
D_EXPERT = D_MODEL // 4
ROPE_BASE = 10000.0
EPS = 1e-6
Q_BLOCK = 128
ATTN_SCALE = QK_DIM ** -0.5

kernel_name = 'hybrid_mla_pool_hmoe_diffusion_step'


def rms_norm(x, w):
    xf = x.astype(jnp.float32)
    y = xf * lax.rsqrt(jnp.mean(xf * xf, axis=-1, keepdims=True) + EPS)
    return (y * w.astype(jnp.float32)).astype(x.dtype)


def grid_rope_tables(n_tokens):
    rows = n_tokens // GRID_W
    row_pos = jnp.repeat(jnp.arange(rows, dtype=jnp.float32), GRID_W)
    col_pos = jnp.tile(jnp.arange(GRID_W, dtype=jnp.float32), rows)
    n_freq = ROPE_DIM // 4
    inv_freq = ROPE_BASE ** (-jnp.arange(n_freq, dtype=jnp.float32) / n_freq)
    ang_r = row_pos[:, None] * inv_freq[None, :]
    ang_c = col_pos[:, None] * inv_freq[None, :]
    return (jnp.cos(ang_r), jnp.sin(ang_r), jnp.cos(ang_c), jnp.sin(ang_c))


def _rotate_axis(x, cos, sin):
    half = x.shape[-1] // 2
    x1, x2 = x[..., :half], x[..., half:]
    cos = cos[None, :, None, :]
    sin = sin[None, :, None, :]
    return jnp.concatenate([x1 * cos - x2 * sin, x2 * cos + x1 * sin], axis=-1)


def apply_grid_rope(x, tables):
    cos_r, sin_r, cos_c, sin_c = tables
    xf = x.astype(jnp.float32)
    nope = xf[..., :NOPE_DIM]
    xr = xf[..., NOPE_DIM:NOPE_DIM + ROPE_DIM // 2]
    xc = xf[..., NOPE_DIM + ROPE_DIM // 2:]
    out = jnp.concatenate([nope, _rotate_axis(xr, cos_r, sin_r), _rotate_axis(xc, cos_c, sin_c)], axis=-1)
    return out.astype(x.dtype)


def block_attention(q, k, v):
    b, h, lq, dq = q.shape
    nb = lq // Q_BLOCK
    qb = q.reshape(b, h, nb, Q_BLOCK, dq).transpose(2, 0, 1, 3, 4)

    def one_block(qi):
        s = jnp.einsum('bhqd,bhkd->bhqk', qi, k).astype(jnp.float32) * ATTN_SCALE
        p = jax.nn.softmax(s, axis=-1)
        return jnp.einsum('bhqk,bhkd->bhqd', p.astype(v.dtype), v)

    o = lax.map(one_block, qb)
    return o.transpose(1, 2, 0, 3, 4).reshape(b, h, lq, V_DIM)


def mla_keys(kv_lat, k_rope, w_kv_up, k_head_norm):
    b, l, _ = kv_lat.shape
    kv = jnp.einsum('blr,rc->blc', kv_lat, w_kv_up).reshape(b, l, N_HEADS, NOPE_DIM + V_DIM)
    k_nope, v = kv[..., :NOPE_DIM], kv[..., NOPE_DIM:]
    k_r = jnp.broadcast_to(k_rope[:, :, None, :], (b, l, N_HEADS, ROPE_DIM))
    k = rms_norm(jnp.concatenate([k_nope, k_r], axis=-1), k_head_norm)
    return k, v


def centred_pool_minus_self(x, window):
    b, l, ch = x.shape
    xf = x.astype(jnp.float32)
    csum = jnp.concatenate([jnp.zeros((b, 1, ch), jnp.float32), jnp.cumsum(xf, axis=1)], axis=1)
    t = jnp.arange(l)
    lo = jnp.clip(t - window // 2, 0, l)
    hi = jnp.clip(t + window // 2, 0, l)
    mean = (csum[:, hi] - csum[:, lo]) / (hi - lo).astype(jnp.float32)[None, :, None]
    return (mean - xf).astype(x.dtype)


def pool_mixer(u, w_pool, pool_scale):
    b, l, _ = u.shape
    ug = u.reshape(b, l, N_POOL_GROUPS, POOL_GROUP_DIM)
    pooled = jnp.stack([centred_pool_minus_self(ug[:, :, g], POOL_WINDOWS[g]) for g in range(N_POOL_GROUPS)], axis=2)
    out = jnp.einsum('blgc,gcd->blgd', pooled, w_pool).reshape(b, l, POOL_WIDTH)
    return out * pool_scale


def hierarchical_moe(h, w_group_router, b_group_router, w_expert_router, b_expert_router, w_gate, w_up, w_down):
    b, l, d = h.shape
    xt = h.reshape(-1, d)
    n_tok = xt.shape[0]
    g_logits = (xt @ w_group_router + b_group_router).astype(jnp.float32)
    g_prob = jax.nn.softmax(g_logits, axis=-1)
    g_top, g_sel = lax.top_k(g_prob, 1)
    e_logits = (xt @ w_expert_router + b_expert_router).astype(jnp.float32)
    e_logits = e_logits.reshape(n_tok, N_EXPERT_GROUPS, EXPERTS_PER_GROUP)[jnp.arange(n_tok), g_sel[:, 0]]
    e_prob = jax.nn.softmax(e_logits, axis=-1)
    e_top, e_idx = lax.top_k(e_prob, TOP_K)
    gates = g_top * e_top / jnp.sum(e_top, axis=-1, keepdims=True)
    expert_id = (g_sel * EXPERTS_PER_GROUP + e_idx).reshape(-1)
    tok = jnp.repeat(jnp.arange(n_tok), TOP_K)
    gate_flat = gates.reshape(-1)
    order = jnp.argsort(expert_id)
    tok_s = tok[order]
    gate_s = gate_flat[order].astype(h.dtype)
    sizes = jnp.bincount(expert_id, length=N_EXPERTS).astype(jnp.int32)
    xs = xt[tok_s]
    hid = jax.nn.silu(lax.ragged_dot(xs, w_gate, sizes)) * lax.ragged_dot(xs, w_up, sizes)
    out = lax.ragged_dot(hid, w_down, sizes) * gate_s[:, None]
    y = jnp.zeros((n_tok, d), h.dtype).at[tok_s].add(out.astype(h.dtype))
    return y.reshape(b, l, d)


def trunk_layer(x, cond, p, rope_tables, ctx_ckv, ctx_krope):
    (w_ada, b_ada, norm_mix, norm_ffn, w_in, q_lora_norm, w_q_up, kv_lora_norm, w_kv_up,
     q_head_norm, k_head_norm, w_pool, pool_scale, attn_out_norm, pool_out_norm, w_out,
     w_group_router, b_group_router, w_expert_router, b_expert_router, w_gate, w_up, w_down) = p
    b, l, _ = x.shape
    mod = jnp.einsum('bd,de->be', jax.nn.silu(cond), w_ada) + b_ada
    shift1, scale1, gate1, shift2, scale2, gate2 = [m[:, None, :] for m in jnp.split(mod, 6, axis=-1)]
    h = rms_norm(x, norm_mix) * (1 + scale1) + shift1
    proj = h @ w_in
    o1 = Q_LORA
    o2 = o1 + KV_LORA
    o3 = o2 + ROPE_DIM
    q_lat = rms_norm(proj[..., :o1], q_lora_norm)
    kv_lat = rms_norm(proj[..., o1:o2], kv_lora_norm)
    k_rope = proj[..., o2:o3]
    u_pool = proj[..., o3:]
    q = rms_norm((q_lat @ w_q_up).reshape(b, l, N_HEADS, QK_DIM), q_head_norm)
    k, v = mla_keys(kv_lat, k_rope, w_kv_up, k_head_norm)
    if rope_tables is not None:
        q = apply_grid_rope(q, rope_tables)
        k = apply_grid_rope(k, rope_tables)
    if ctx_ckv is not None:
        k_ctx, v_ctx = mla_keys(ctx_ckv, ctx_krope, w_kv_up, k_head_norm)
        k = jnp.concatenate([k_ctx, k], axis=1)
        v = jnp.concatenate([v_ctx, v], axis=1)
    attn = block_attention(q.transpose(0, 2, 1, 3), k.transpose(0, 2, 1, 3), v.transpose(0, 2, 1, 3))
    attn = attn.transpose(0, 2, 1, 3).reshape(b, l, ATTN_WIDTH)
    pool = pool_mixer(u_pool, w_pool, pool_scale)
    mixed = jnp.concatenate([rms_norm(attn, attn_out_norm), rms_norm(pool, pool_out_norm)], axis=-1) @ w_out
    x = x + gate1 * mixed
    h2 = rms_norm(x, norm_ffn) * (1 + scale2) + shift2
    x = x + gate2 * hierarchical_moe(h2, w_group_router, b_group_router, w_expert_router, b_expert_router,
                                     w_gate, w_up, w_down)
    return x, kv_lat, k_rope


def setup_inputs(seed: int = 0) -> dict:
    key = jax.random.key(seed)
    ks = jax.random.split(key, 32)

    def nrm(k, shape, scale):
        return jax.random.normal(k, shape, jnp.float32) * scale

    L = DEPTH
    return {
        'x_prompt': nrm(ks[0], (BATCH, SEQ, D_MODEL), 1.0),
        'x_sample': nrm(ks[1], (DEC_BATCH, DEC_SEQ, D_MODEL), 1.0),
        'cache_ckv': nrm(ks[2], (DEC_BATCH, DEPTH, PAST_LEN, KV_LORA), 1.0),
        'cache_krope': nrm(ks[3], (DEC_BATCH, DEPTH, PAST_LEN, ROPE_DIM), 1.0),
        'c': nrm(ks[4], (DEC_BATCH, D_MODEL), 1.0),
        'c_ctx': nrm(ks[5], (D_MODEL,), 1.0),
        'w_ada': nrm(ks[6], (L, D_MODEL, 6 * D_MODEL), D_MODEL ** -0.5),
        'b_ada': nrm(ks[7], (L, 6 * D_MODEL), 0.01),
        'norm_mix': 1.0 + nrm(ks[8], (L, D_MODEL), 0.02),
        'norm_ffn': 1.0 + nrm(ks[9], (L, D_MODEL), 0.02),
        'w_in': nrm(ks[10], (L, D_MODEL, IN_COLS), D_MODEL ** -0.5),
        'q_lora_norm': 1.0 + nrm(ks[11], (L, Q_LORA), 0.02),
        'w_q_up': nrm(ks[12], (L, Q_LORA, N_HEADS * QK_DIM), Q_LORA ** -0.5),
        'kv_lora_norm': 1.0 + nrm(ks[13], (L, KV_LORA), 0.02),
        'w_kv_up': nrm(ks[14], (L, KV_LORA, N_HEADS * (NOPE_DIM + V_DIM)), KV_LORA ** -0.5),
        'q_head_norm': 1.0 + nrm(ks[15], (L, QK_DIM), 0.02),
        'k_head_norm': 1.0 + nrm(ks[16], (L, QK_DIM), 0.02),
        'w_pool': nrm(ks[17], (L, N_POOL_GROUPS, POOL_GROUP_DIM, POOL_GROUP_DIM), POOL_GROUP_DIM ** -0.5),
        'pool_scale': 1.0 + nrm(ks[18], (L, POOL_WIDTH), 0.02),
        'attn_out_norm': 1.0 + nrm(ks[19], (L, ATTN_WIDTH), 0.02),
        'pool_out_norm': 1.0 + nrm(ks[20], (L, POOL_WIDTH), 0.02),
        'w_out': nrm(ks[21], (L, D_MODEL, D_MODEL), D_MODEL ** -0.5),
        'w_group_router': nrm(ks[22], (L, D_MODEL, N_EXPERT_GROUPS), D_MODEL ** -0.5),
        'b_group_router': nrm(ks[23], (L, N_EXPERT_GROUPS), 0.01),
        'w_expert_router': nrm(ks[24], (L, D_MODEL, N_EXPERTS), D_MODEL ** -0.5),
        'b_expert_router': nrm(ks[25], (L, N_EXPERTS), 0.01),
        'w_gate': nrm(ks[26], (L, N_EXPERTS, D_MODEL, D_EXPERT), D_MODEL ** -0.5),
        'w_up': nrm(ks[27], (L, N_EXPERTS, D_MODEL, D_EXPERT), D_MODEL ** -0.5),
        'w_down': nrm(ks[28], (L, N_EXPERTS, D_EXPERT, D_MODEL), D_EXPERT ** -0.5),
    }


def reference(x_prompt, x_sample, cache_ckv, cache_krope, c, c_ctx, w_ada, b_ada, norm_mix, norm_ffn, w_in,
              q_lora_norm, w_q_up, kv_lora_norm, w_kv_up, q_head_norm, k_head_norm, w_pool, pool_scale,
              attn_out_norm, pool_out_norm, w_out, w_group_router, b_group_router, w_expert_router,
              b_expert_router, w_gate, w_up, w_down):
    rope_tables = grid_rope_tables(x_sample.shape[1])
    ctx_cond = c_ctx[None, :]
    y_p = x_prompt
    y_s = x_sample
    ckv_list = []
    krope_list = []
    for l in range(DEPTH):
        p = (w_ada[l], b_ada[l], norm_mix[l], norm_ffn[l], w_in[l], q_lora_norm[l], w_q_up[l], kv_lora_norm[l],
             w_kv_up[l], q_head_norm[l], k_head_norm[l], w_pool[l], pool_scale[l], attn_out_norm[l],
             pool_out_norm[l], w_out[l], w_group_router[l], b_group_router[l], w_expert_router[l],
             b_expert_router[l], w_gate[l], w_up[l], w_down[l])
        y_p, ckv_l, krope_l = trunk_layer(y_p, ctx_cond, p, None, None, None)
        ckv_list.append(ckv_l)
        krope_list.append(krope_l)
        y_s, _, _ = trunk_layer(y_s, c, p, rope_tables, cache_ckv[:, l], cache_krope[:, l])
    new_ckv = jnp.stack(ckv_list, axis=1)
    new_krope = jnp.stack(krope_list, axis=1)
    return (y_p, y_s, new_ckv, new_krope)
```

```python
import functools

import numpy as np
import jax
import jax.numpy as jnp
from jax import lax
from jax.experimental import pallas as pl
from jax.experimental.pallas import tpu as pltpu

F32 = jnp.float32
BF16 = jnp.bfloat16

D_MODEL = 2048
BATCH = 32
SEQ = 256
DEPTH = 2
DEC_BATCH = 4
DEC_SEQ = 1024
PAST_LEN = 512
GRID_W = 64
N_HEADS = 8
NOPE_DIM = 128
ROPE_DIM = 64
QK_DIM = NOPE_DIM + ROPE_DIM
V_DIM = 128
Q_LORA = 512
KV_LORA = 256
ATTN_WIDTH = N_HEADS * V_DIM
POOL_WIDTH = D_MODEL - ATTN_WIDTH
POOL_WINDOWS = (2, 4, 8, 16)
N_POOL_GROUPS = 4
POOL_GROUP_DIM = POOL_WIDTH // N_POOL_GROUPS
N_EXPERT_GROUPS = 4
EXPERTS_PER_GROUP = 8
N_EXPERTS = 32
D_EXPERT = 512
ROPE_BASE = 10000.0
EPS = 1e-6
ATTN_SCALE = QK_DIM ** -0.5

LANES = 128
TM = 256
N_CTX_TOK = BATCH * SEQ
N_S_TOK = DEC_BATCH * DEC_SEQ
N_TOK = N_CTX_TOK + N_S_TOK
N_CTX_TILES = N_CTX_TOK // TM
SEQ_TILES_S = DEC_SEQ // TM
N_TILES = N_TOK // TM
HEAD_PAD = 256
QK_WIDTH = N_HEADS * HEAD_PAD
IN_COLS_PAD = Q_LORA + KV_LORA + LANES + POOL_WIDTH
HALO = 64
N_ROWS_SORTED = 2 * N_TOK
N_SORT_TILES = N_ROWS_SORTED // TM
MAX_ITEMS = N_SORT_TILES + N_EXPERTS - 1
VMEM_LIMIT = 56 * 1024 * 1024


def _cparams(n_axes=1):
    return pltpu.CompilerParams(dimension_semantics=("arbitrary",) * n_axes,
                                vmem_limit_bytes=VMEM_LIMIT)


def _const_spec(shape):
    nd = len(shape)
    return pl.BlockSpec(shape, lambda *_: (0,) * nd, pipeline_mode=pl.Buffered(1))


def _mod_row(i):
    return jnp.where(i < N_CTX_TILES, 0, 1 + (i - N_CTX_TILES) // SEQ_TILES_S)


def _rms(x, w, inv_n=None):
    if inv_n is None:
        ms = jnp.mean(x * x, axis=-1, keepdims=True)
    else:
        ms = jnp.sum(x * x, axis=-1, keepdims=True) * inv_n
    return x * lax.rsqrt(ms + EPS) * w


ADA_TN = 1024


def _ada_kernel(cond_ref, w_ref, b_ref, o_ref):
    c = cond_ref[...]
    s = (c * jax.nn.sigmoid(c)).astype(BF16)
    o_ref[...] = jnp.dot(s, w_ref[...].astype(BF16), preferred_element_type=F32) + b_ref[...]


def _ada_mod(cond8, w_ada, b_ada):
    n_out = 6 * D_MODEL
    return pl.pallas_call(
        _ada_kernel,
        grid=(DEPTH, n_out // ADA_TN),
        in_specs=[
            pl.BlockSpec((8, D_MODEL), lambda l, j: (0, 0)),
            pl.BlockSpec((None, D_MODEL, ADA_TN), lambda l, j: (l, 0, j)),
            pl.BlockSpec((None, 1, ADA_TN), lambda l, j: (l, 0, j)),
        ],
        out_specs=pl.BlockSpec((None, 8, ADA_TN), lambda l, j: (l, 0, j)),
        out_shape=jax.ShapeDtypeStruct((DEPTH, 8, n_out), F32),
        compiler_params=_cparams(2),
        name="ada_mod",
    )(cond8, w_ada, b_ada.reshape(DEPTH, 1, n_out))


def _rope(p, cos, sin):
    lane = lax.broadcasted_iota(jnp.int32, p.shape, 1)
    first_half = (lane % 32) < 16
    partner = jnp.where(first_half, pltpu.roll(p, LANES - 16, 1), pltpu.roll(p, 16, 1))
    return p * cos + partner * sin


def _write_keys(kn_all, kr, khn, cos, sin, k_ref):
    krw = kr * khn[:, NOPE_DIM:]
    if cos is not None:
        krw = _rope(krw, cos, sin)
    kr_ss = jnp.sum(kr * kr, axis=-1, keepdims=True)
    for h in range(N_HEADS):
        kn = kn_all[:, h * NOPE_DIM:(h + 1) * NOPE_DIM]
        ss = jnp.sum(kn * kn, axis=-1, keepdims=True) + kr_ss
        r = lax.rsqrt(ss * (1.0 / QK_DIM) + EPS)
        k_ref[:, h * HEAD_PAD:h * HEAD_PAD + NOPE_DIM] = (kn * r * khn[:, :NOPE_DIM]).astype(BF16)
        k_ref[:, h * HEAD_PAD + NOPE_DIM:(h + 1) * HEAD_PAD] = (krw * r).astype(BF16)


def _in_kernel(x_ref, mod_ref, nmix_ref, win_ref, qln_ref, wq_ref, kvn_ref, wkv_ref, qhn_ref, khn_ref,
               cos_ref, sin_ref, q_ref, k_ref, v_ref, ckv_ref, kr_ref, u_ref):
    x = x_ref[...]
    h = _rms(x, nmix_ref[...]) * (1.0 + mod_ref[1:2, :]) + mod_ref[0:1, :]
    proj = jnp.dot(h.astype(BF16), win_ref[...], preferred_element_type=F32)
    o1 = Q_LORA
    o2 = o1 + KV_LORA
    o3 = o2 + LANES
    q_lat = _rms(proj[:, :o1], qln_ref[...])
    kv_lat = _rms(proj[:, o1:o2], kvn_ref[...])
    kr = proj[:, o2:o3]
    ckv_ref[...] = kv_lat
    kr_ref[...] = kr[:, :ROPE_DIM]
    u_ref[...] = proj[:, o3:].astype(BF16)

    q = jnp.dot(q_lat.astype(BF16), wq_ref[...], preferred_element_type=F32)
    kv = jnp.dot(kv_lat.astype(BF16), wkv_ref[...], preferred_element_type=F32)
    v_ref[...] = kv[:, N_HEADS * NOPE_DIM:].astype(BF16)

    cos = cos_ref[...]
    sin = sin_ref[...]
    qhn = qhn_ref[...]
    for hd in range(N_HEADS):
        blk = q[:, hd * HEAD_PAD:(hd + 1) * HEAD_PAD]
        ss = jnp.sum(blk * blk, axis=-1, keepdims=True)
        qn = blk * lax.rsqrt(ss * (1.0 / QK_DIM) + EPS) * qhn
        q_ref[:, hd * HEAD_PAD:hd * HEAD_PAD + NOPE_DIM] = qn[:, :NOPE_DIM].astype(BF16)
        q_ref[:, hd * HEAD_PAD + NOPE_DIM:(hd + 1) * HEAD_PAD] = _rope(qn[:, NOPE_DIM:], cos, sin).astype(BF16)
    _write_keys(kv[:, :N_HEADS * NOPE_DIM], kr, khn_ref[...], cos, sin, k_ref)


def _in_proj(x, mod, lw, rope_cos, rope_sin, layer):
    def tile(i):
        return (i, 0)

    def rope_blk(i):
        return (jnp.where(i < N_CTX_TILES, 0, 1 + (i - N_CTX_TILES) % SEQ_TILES_S), 0)

    return pl.pallas_call(
        _in_kernel,
        grid=(N_TILES,),
        in_specs=[
            pl.BlockSpec((TM, D_MODEL), tile),
            pl.BlockSpec((None, 6, D_MODEL), lambda i: (layer * 8 + _mod_row(i), 0, 0)),
            _const_spec((1, D_MODEL)),
            _const_spec((D_MODEL, IN_COLS_PAD)),
            _const_spec((1, Q_LORA)),
            _const_spec((Q_LORA, QK_WIDTH)),
            _const_spec((1, KV_LORA)),
            _const_spec((KV_LORA, 2 * N_HEADS * NOPE_DIM)),
            _const_spec((1, HEAD_PAD)),
            _const_spec((1, HEAD_PAD)),
            pl.BlockSpec((TM, LANES), rope_blk),
            pl.BlockSpec((TM, LANES), rope_blk),
        ],
        out_specs=[
            pl.BlockSpec((TM, QK_WIDTH), tile),
            pl.BlockSpec((TM, QK_WIDTH), tile),
            pl.BlockSpec((TM, ATTN_WIDTH), tile),
            pl.BlockSpec((TM, KV_LORA), tile),
            pl.BlockSpec((TM, ROPE_DIM), tile),
            pl.BlockSpec((TM, POOL_WIDTH), tile),
        ],
        out_shape=[
            jax.ShapeDtypeStruct((N_TOK, QK_WIDTH), BF16),
            jax.ShapeDtypeStruct((N_TOK, QK_WIDTH), BF16),
            jax.ShapeDtypeStruct((N_TOK, ATTN_WIDTH), BF16),
            jax.ShapeDtypeStruct((N_TOK, KV_LORA), F32),
            jax.ShapeDtypeStruct((N_TOK, ROPE_DIM), F32),
            jax.ShapeDtypeStruct((N_TOK, POOL_WIDTH), BF16),
        ],
        compiler_params=_cparams(),
        name="in_proj",
    )(x, mod, lw["norm_mix"], lw["w_in"], lw["q_lora_norm"], lw["w_q"], lw["kv_lora_norm"], lw["w_kv"],
      lw["q_head_norm"], lw["k_head_norm"], rope_cos, rope_sin)


def _cache_keys_kernel(ckv_ref, kr_ref, wkv_ref, khn_ref, k_ref, v_ref):
    kv = jnp.dot(ckv_ref[...].astype(BF16), wkv_ref[...], preferred_element_type=F32)
    v_ref[...] = kv[:, N_HEADS * NOPE_DIM:].astype(BF16)
    _write_keys(kv[:, :N_HEADS * NOPE_DIM], kr_ref[...], khn_ref[...], None, None, k_ref)


def _cache_keys(ckv, kr_pad, lw):
    n = ckv.shape[0]
    return pl.pallas_call(
        _cache_keys_kernel,
        grid=(n // TM,),
        in_specs=[
            pl.BlockSpec((TM, KV_LORA), lambda i: (i, 0)),
            pl.BlockSpec((TM, LANES), lambda i: (i, 0)),
            _const_spec((KV_LORA, 2 * N_HEADS * NOPE_DIM)),
            _const_spec((1, HEAD_PAD)),
        ],
        out_specs=[
            pl.BlockSpec((TM, QK_WIDTH), lambda i: (i, 0)),
            pl.BlockSpec((TM, ATTN_WIDTH), lambda i: (i, 0)),
        ],
        out_shape=[
            jax.ShapeDtypeStruct((n, QK_WIDTH), BF16),
            jax.ShapeDtypeStruct((n, ATTN_WIDTH), BF16),
        ],
        compiler_params=_cparams(),
        name="cache_keys",
    )(ckv, kr_pad, lw["w_kv"], lw["k_head_norm"])


_NT = (((1,), (1,)), ((), ()))


def _attn_kernel(q_ref, kx_ref, vx_ref, kc_ref, vc_ref, ks_ref, vs_ref, o_ref):
    i = pl.program_id(0)

    @pl.when(i < N_CTX_TILES)
    def _():
        for h in range(N_HEADS):
            cs = slice(h * HEAD_PAD, (h + 1) * HEAD_PAD)
            vs = slice(h * V_DIM, (h + 1) * V_DIM)
            s = lax.dot_general(q_ref[:, cs], kx_ref[:, cs], _NT, preferred_element_type=F32)
            p = jnp.exp(s - jnp.max(s, axis=-1, keepdims=True))
            den = jnp.sum(p, axis=-1, keepdims=True)
            o = jnp.dot(p.astype(BF16), vx_ref[:, vs], preferred_element_type=F32) / den
            o_ref[:, vs] = o.astype(BF16)

    @pl.when(i >= N_CTX_TILES)
    def _():
        for h in range(N_HEADS):
            cs = slice(h * HEAD_PAD, (h + 1) * HEAD_PAD)
            vs = slice(h * V_DIM, (h + 1) * V_DIM)
            q = q_ref[:, cs]
            s1 = lax.dot_general(q, kc_ref[:, cs], _NT, preferred_element_type=F32)
            s2 = lax.dot_general(q, ks_ref[:, cs], _NT, preferred_element_type=F32)
            m = jnp.maximum(jnp.max(s1, axis=-1, keepdims=True), jnp.max(s2, axis=-1, keepdims=True))
            p1 = jnp.exp(s1 - m)
            p2 = jnp.exp(s2 - m)
            den = jnp.sum(p1, axis=-1, keepdims=True) + jnp.sum(p2, axis=-1, keepdims=True)
            o = (jnp.dot(p1.astype(BF16), vc_ref[:, vs], preferred_element_type=F32)
                 + jnp.dot(p2.astype(BF16), vs_ref[:, vs], preferred_element_type=F32)) / den
            o_ref[:, vs] = o.astype(BF16)


def _attention(q, k, v, kc, vc):
    def ctx_blk(i):
        return (jnp.minimum(i, N_CTX_TILES - 1), 0)

    def s_batch(i):
        return jnp.maximum(i - N_CTX_TILES, 0) // SEQ_TILES_S

    s_blk0 = N_CTX_TOK // DEC_SEQ
    return pl.pallas_call(
        _attn_kernel,
        grid=(N_TILES,),
        in_specs=[
            pl.BlockSpec((TM, QK_WIDTH), lambda i: (i, 0)),
            pl.BlockSpec((TM, QK_WIDTH), ctx_blk),
            pl.BlockSpec((TM, ATTN_WIDTH), ctx_blk),
            pl.BlockSpec((PAST_LEN, QK_WIDTH), lambda i: (s_batch(i), 0)),
            pl.BlockSpec((PAST_LEN, ATTN_WIDTH), lambda i: (s_batch(i), 0)),
            pl.BlockSpec((DEC_SEQ, QK_WIDTH), lambda i: (s_blk0 + s_batch(i), 0)),
            pl.BlockSpec((DEC_SEQ, ATTN_WIDTH), lambda i: (s_blk0 + s_batch(i), 0)),
        ],
        out_specs=pl.BlockSpec((TM, ATTN_WIDTH), lambda i: (i, 0)),
        out_shape=jax.ShapeDtypeStruct((N_TOK, ATTN_WIDTH), BF16),
        compiler_params=_cparams(),
        name="attention",
    )(q, k, v, kc, vc, k, v)


SLAB_E0, SLAB_E1, SLAB_G0, SLAB_G1, SLAB_R0, SLAB_R1 = range(6)
ROUTER_GROUP_LANES = N_EXPERT_GROUPS
NEG_BIG = -1e30


def _first_argmax(vals, lane):
    m = jnp.max(vals, axis=-1, keepdims=True)
    idx = jnp.min(jnp.where(vals == m, lane, float(LANES)), axis=-1, keepdims=True)
    return m, idx


def _mix_kernel(x_ref, attn_ref, u_ref, up_ref, un_ref, mod_ref, aon_ref, pon_ref, psc_ref, wpool_ref,
                band_ref, wout_ref, nffn_ref, wrh_ref, wrl_ref, br_ref, tri_ref,
                x1_ref, h2_ref, slab_ref, cnt_ref, carry_ref):
    i = pl.program_id(0)
    is_ctx = i < N_CTX_TILES
    pos = jnp.where(is_ctx, 0, (i - N_CTX_TILES) % SEQ_TILES_S)
    seq_tiles = jnp.where(is_ctx, 1, SEQ_TILES_S)
    seq_len = seq_tiles * TM

    a = attn_ref[...].astype(F32)
    a_n = _rms(a, aon_ref[...]).astype(BF16)

    u = u_ref[...]
    up = jnp.where(pos == 0, jnp.zeros_like(up_ref[...]), up_ref[...])
    un = jnp.where(pos == seq_tiles - 1, jnp.zeros_like(un_ref[...]), un_ref[...])
    ext = jnp.concatenate([up, u, un], axis=0)
    t = pos * TM + lax.broadcasted_iota(jnp.int32, (TM, 1), 0)
    pools = []
    for g in range(N_POOL_GROUPS):
        half = POOL_WINDOWS[g] // 2
        gs = slice(g * POOL_GROUP_DIM, (g + 1) * POOL_GROUP_DIM)
        wsum = jnp.dot(band_ref[g], ext[:, gs], preferred_element_type=F32)
        cnt = jnp.minimum(t + half, seq_len) - jnp.maximum(t - half, 0)
        pooled = wsum / cnt.astype(F32) - u[:, gs].astype(F32)
        pools.append(jnp.dot(pooled.astype(BF16), wpool_ref[g], preferred_element_type=F32))
    pool = jnp.concatenate(pools, axis=-1) * psc_ref[...]
    p_n = _rms(pool, pon_ref[...]).astype(BF16)

    mixed = (jnp.dot(a_n, wout_ref[:ATTN_WIDTH, :], preferred_element_type=F32)
             + jnp.dot(p_n, wout_ref[ATTN_WIDTH:, :], preferred_element_type=F32))
    x1 = x_ref[...] + mod_ref[2:3, :] * mixed
    x1_ref[...] = x1
    h2 = _rms(x1, nffn_ref[...]) * (1.0 + mod_ref[4:5, :]) + mod_ref[3:4, :]

    hi = h2.astype(BF16)
    hi_f = hi.astype(F32)
    half_w = D_MODEL // 2
    bits = pltpu.bitcast(hi_f, jnp.uint32)
    h2_ref[...] = bits[:, :half_w] | lax.shift_right_logical(bits[:, half_w:], jnp.uint32(16))

    lo = (h2 - hi_f).astype(BF16)
    logits = (jnp.dot(hi, wrh_ref[...], preferred_element_type=F32)
              + jnp.dot(lo, wrh_ref[...], preferred_element_type=F32)
              + jnp.dot(hi, wrl_ref[...], preferred_element_type=F32)) + br_ref[...]

    lane = lax.broadcasted_iota(jnp.int32, (TM, LANES), 1).astype(F32)
    is_g = lane < ROUTER_GROUP_LANES
    gl = jnp.where(is_g, logits, NEG_BIG)
    gmax, gsel = _first_argmax(gl, lane)
    gsum = jnp.sum(jnp.where(is_g, jnp.exp(gl - gmax), 0.0), axis=-1, keepdims=True)
    g_top = 1.0 / gsum
    base = ROUTER_GROUP_LANES + EXPERTS_PER_GROUP * gsel
    el = jnp.where(lane >= base, jnp.where(lane < base + EXPERTS_PER_GROUP, logits, NEG_BIG), NEG_BIG)
    e1, i1 = _first_argmax(el, lane)
    el2 = jnp.where(lane == i1, NEG_BIG, el)
    e2, i2 = _first_argmax(el2, lane)
    r21 = jnp.exp(e2 - e1)
    gate0 = g_top / (1.0 + r21)
    gate1 = g_top * r21 / (1.0 + r21)
    eid0 = i1 - ROUTER_GROUP_LANES
    eid1 = i2 - ROUTER_GROUP_LANES

    @pl.when(i == 0)
    def _():
        carry_ref[...] = jnp.zeros_like(carry_ref)

    sel0 = lane == eid0
    sel1 = lane == eid1
    onehot = jnp.where(sel0, 1.0, jnp.where(sel1, 1.0, 0.0))
    before = jnp.dot(tri_ref[...], onehot.astype(BF16), preferred_element_type=F32) + carry_ref[...]
    rank0 = jnp.sum(jnp.where(sel0, before, 0.0), axis=-1, keepdims=True)
    rank1 = jnp.sum(jnp.where(sel1, before, 0.0), axis=-1, keepdims=True)
    carry_ref[...] = carry_ref[...] + jnp.sum(onehot, axis=0, keepdims=True)
    cnt_ref[...] = jnp.broadcast_to(carry_ref[...], cnt_ref.shape)

    slab = jnp.where(lane == SLAB_E0, eid0.astype(F32),
           jnp.where(lane == SLAB_E1, eid1.astype(F32),
           jnp.where(lane == SLAB_G0, gate0,
           jnp.where(lane == SLAB_G1, gate1,
           jnp.where(lane == SLAB_R0, rank0,
           jnp.where(lane == SLAB_R1, rank1, 0.0))))))
    slab_ref[...] = slab


def _mix(x, attn, u, mod, lw, band, tri, layer):
    halo_per_tile = TM // HALO
    n_halo_blocks = N_TOK // HALO

    def tile(i):
        return (i, 0)

    return pl.pallas_call(
        _mix_kernel,
        grid=(N_TILES,),
        in_specs=[
            pl.BlockSpec((TM, D_MODEL), tile),
            pl.BlockSpec((TM, ATTN_WIDTH), tile),
            pl.BlockSpec((TM, POOL_WIDTH), tile),
            pl.BlockSpec((HALO, POOL_WIDTH), lambda i: (jnp.maximum(i * halo_per_tile - 1, 0), 0)),
            pl.BlockSpec((HALO, POOL_WIDTH),
                         lambda i: (jnp.minimum((i + 1) * halo_per_tile, n_halo_blocks - 1), 0)),
            pl.BlockSpec((None, 6, D_MODEL), lambda i: (layer * 8 + _mod_row(i), 0, 0)),
            _const_spec((1, ATTN_WIDTH)),
            _const_spec((1, POOL_WIDTH)),
            _const_spec((1, POOL_WIDTH)),
            _const_spec((N_POOL_GROUPS, POOL_GROUP_DIM, POOL_GROUP_DIM)),
            _const_spec((N_POOL_GROUPS, TM, TM + 2 * HALO)),
            _const_spec((D_MODEL, D_MODEL)),
            _const_spec((1, D_MODEL)),
            _const_spec((D_MODEL, LANES)),
            _const_spec((D_MODEL, LANES)),
            _const_spec((1, LANES)),
            _const_spec((TM, TM)),
        ],
        out_specs=[
            pl.BlockSpec((TM, D_MODEL), tile),
            pl.BlockSpec((TM, D_MODEL // 2), tile),
            pl.BlockSpec((TM, LANES), tile),
            pl.BlockSpec((8, LANES), lambda i: (0, 0)),
        ],
        out_shape=[
            jax.ShapeDtypeStruct((N_TOK, D_MODEL), F32),
            jax.ShapeDtypeStruct((N_TOK, D_MODEL // 2), jnp.uint32),
            jax.ShapeDtypeStruct((N_TOK, LANES), F32),
            jax.ShapeDtypeStruct((8, LANES), F32),
        ],
        scratch_shapes=[pltpu.VMEM((1, LANES), F32)],
        compiler_params=_cparams(),
        name="mix_router",
    )(x, attn, u, u, u, mod, lw["attn_out_norm"], lw["pool_out_norm"], lw["pool_scale"], lw["w_pool"], band,
      lw["w_out"], lw["norm_ffn"], lw["w_router_hi"], lw["w_router_lo"], lw["b_router"], tri)


def _row_copy(src_ref, dst_ref, src_row, dst_row, sem):
    return pltpu.make_async_copy(src_ref.at[pl.ds(src_row, 1), :], dst_ref.at[pl.ds(dst_row, 1), :], sem)


def _dispatch_kernel(pos_ref, h2_ref, xs_ref, sem):
    def issue(r, c):
        _row_copy(h2_ref, xs_ref, r, pos_ref[0, 2 * r], sem).start()
        _row_copy(h2_ref, xs_ref, r, pos_ref[0, 2 * r + 1], sem).start()
        return c

    lax.fori_loop(0, TM, issue, 0)

    def drain(r, c):
        _row_copy(h2_ref, xs_ref, 0, 0, sem).wait()
        _row_copy(h2_ref, xs_ref, 0, 0, sem).wait()
        return c

    lax.fori_loop(0, TM, drain, 0)


def _dispatch(h2p, pos):
    return pl.pallas_call(
        _dispatch_kernel,
        grid=(N_TILES,),
        in_specs=[
            pl.BlockSpec((None, 1, 2 * TM), lambda i: (i, 0, 0), memory_space=pltpu.SMEM),
            pl.BlockSpec((TM, D_MODEL // 2), lambda i: (i, 0)),
        ],
        out_specs=pl.BlockSpec(memory_space=pl.ANY),
        out_shape=jax.ShapeDtypeStruct((N_ROWS_SORTED, D_MODEL // 2), jnp.uint32),
        scratch_shapes=[pltpu.SemaphoreType.DMA(())],
        compiler_params=_cparams(),
        name="dispatch",
    )(pos, h2p)


def _moe_kernel(tile_ref, exp_ref, lo_ref, hi_ref, flag_ref, xs_ref, wg_ref, wu_ref, wd_ref, o_ref,
                wg_b, wu_b, wd_b):
    i = pl.program_id(0)
    flags = flag_ref[i]

    @pl.when((flags & 4) != 0)
    def _():
        wg_b[...] = wg_ref[...].astype(BF16)
        wu_b[...] = wu_ref[...].astype(BF16)
        wd_b[...] = wd_ref[...].astype(BF16)

    @pl.when((flags & 1) != 0)
    def _():
        xp = xs_ref[...]
        xa = pltpu.bitcast(xp & jnp.uint32(0xFFFF0000), F32)
        xb = pltpu.bitcast(lax.shift_left(xp, jnp.uint32(16)), F32)
        x = jnp.concatenate([xa, xb], axis=-1).astype(BF16)
        g = jnp.dot(x, wg_b[...], preferred_element_type=F32)
        up = jnp.dot(x, wu_b[...], preferred_element_type=F32)
        hid = (g * jax.nn.sigmoid(g) * up).astype(BF16)
        o = jnp.dot(hid, wd_b[...], preferred_element_type=F32)
        row = lax.broadcasted_iota(jnp.int32, (TM, D_MODEL), 0)
        mine = (row >= lo_ref[i]) & (row < hi_ref[i])

        @pl.when((flags & 2) != 0)
        def _():
            o_ref[...] = jnp.where(mine, o, 0.0)

        @pl.when((flags & 2) == 0)
        def _():
            o_ref[...] = jnp.where(mine, o, o_ref[...])


def _moe(items, xs, w_gate, w_up, w_down, layer):
    tile_i, exp_i, lo_i, hi_i, flag_i = items
    grid_spec = pltpu.PrefetchScalarGridSpec(
        num_scalar_prefetch=5,
        grid=(MAX_ITEMS,),
        in_specs=[
            pl.BlockSpec((TM, D_MODEL // 2), lambda i, t, e, lo, hi, f: (t[i], 0)),
            pl.BlockSpec((None, None, D_MODEL, D_EXPERT), lambda i, t, e, lo, hi, f: (layer, e[i], 0, 0)),
            pl.BlockSpec((None, None, D_MODEL, D_EXPERT), lambda i, t, e, lo, hi, f: (layer, e[i], 0, 0)),
            pl.BlockSpec((None, None, D_EXPERT, D_MODEL), lambda i, t, e, lo, hi, f: (layer, e[i], 0, 0)),
        ],
        out_specs=pl.BlockSpec((TM, D_MODEL), lambda i, t, e, lo, hi, f: (t[i], 0)),
        scratch_shapes=[
            pltpu.VMEM((D_MODEL, D_EXPERT), BF16),
            pltpu.VMEM((D_MODEL, D_EXPERT), BF16),
            pltpu.VMEM((D_EXPERT, D_MODEL), BF16),
        ],
    )
    return pl.pallas_call(
        _moe_kernel,
        grid_spec=grid_spec,
        out_shape=jax.ShapeDtypeStruct((N_ROWS_SORTED, D_MODEL), F32),
        compiler_params=_cparams(),
        name="expert_mlp",
    )(tile_i, exp_i, lo_i, hi_i, flag_i, xs, w_gate, w_up, w_down)


def _work_items(counts):
    seg_end = jnp.cumsum(counts)
    seg_off = seg_end - counts
    first_tile = seg_off // TM
    last_tile = (seg_end - 1) // TM
    n_it = jnp.where(counts > 0, last_tile - first_tile + 1, 0)
    it_end = jnp.cumsum(n_it)
    it_off = it_end - n_it
    total = it_end[-1]
    idx = jnp.arange(MAX_ITEMS, dtype=jnp.int32)
    valid = idx < total
    idxc = jnp.minimum(idx, total - 1)
    e = jnp.sum((it_end[None, :] <= idxc[:, None]).astype(jnp.int32), axis=-1)
    onehot_e = (e[:, None] == jnp.arange(N_EXPERTS, dtype=jnp.int32)[None, :]).astype(jnp.int32)

    def pick(v):
        return jnp.sum(onehot_e * v[None, :], axis=-1)

    tile = pick(first_tile) + idxc - pick(it_off)
    lo = jnp.maximum(pick(seg_off), tile * TM) - tile * TM
    hi = jnp.minimum(pick(seg_end), (tile + 1) * TM) - tile * TM
    prev_tile = jnp.concatenate([jnp.full((1,), -1, jnp.int32), tile[:-1]])
    prev_e = jnp.concatenate([jnp.full((1,), -1, jnp.int32), e[:-1]])
    flags = (valid.astype(jnp.int32)
             + 2 * (valid & (tile != prev_tile)).astype(jnp.int32)
             + 4 * (valid & (e != prev_e)).astype(jnp.int32))
    lo = jnp.where(valid, lo, 0)
    hi = jnp.where(valid, hi, 0)
    i32 = lambda a: a.astype(jnp.int32)
    return i32(tile), i32(e), i32(lo), i32(hi), i32(flags), seg_off


def _combine_kernel(pos_ref, x1_ref, slab_ref, mod_ref, ys_ref, o_ref, buf0, buf1, sem):
    def issue(r, c):
        _row_copy(ys_ref, buf0, pos_ref[0, 2 * r], r, sem).start()
        _row_copy(ys_ref, buf1, pos_ref[0, 2 * r + 1], r, sem).start()
        return c

    lax.fori_loop(0, TM, issue, 0)

    def drain(r, c):
        _row_copy(ys_ref, buf0, 0, 0, sem).wait()
        _row_copy(ys_ref, buf1, 0, 0, sem).wait()
        return c

    lax.fori_loop(0, TM, drain, 0)
    slab = slab_ref[...]
    g0 = slab[:, SLAB_G0:SLAB_G0 + 1]
    g1 = slab[:, SLAB_G1:SLAB_G1 + 1]
    y = buf0[...] * g0 + buf1[...] * g1
    o_ref[...] = x1_ref[...] + mod_ref[5:6, :] * y


def _combine(pos, x1, slab, mod, ys, layer):
    return pl.pallas_call(
        _combine_kernel,
        grid=(N_TILES,),
        in_specs=[
            pl.BlockSpec((None, 1, 2 * TM), lambda i: (i, 0, 0), memory_space=pltpu.SMEM),
            pl.BlockSpec((TM, D_MODEL), lambda i: (i, 0)),
            pl.BlockSpec((TM, LANES), lambda i: (i, 0)),
            pl.BlockSpec((None, 6, D_MODEL), lambda i: (layer * 8 + _mod_row(i), 0, 0)),
            pl.BlockSpec(memory_space=pl.ANY),
        ],
        out_specs=pl.BlockSpec((TM, D_MODEL), lambda i: (i, 0)),
        out_shape=jax.ShapeDtypeStruct((N_TOK, D_MODEL), F32),
        scratch_shapes=[
            pltpu.VMEM((TM, D_MODEL), F32),
            pltpu.VMEM((TM, D_MODEL), F32),
            pltpu.SemaphoreType.DMA(()),
        ],
        compiler_params=_cparams(),
        name="combine",
    )(pos, x1, slab, mod, ys)


def _rope_tables():
    n_freq = ROPE_DIM // 4
    t = jnp.arange(DEC_SEQ)
    row_pos = (t // GRID_W).astype(F32)
    col_pos = (t % GRID_W).astype(F32)
    inv_freq = ROPE_BASE ** (-jnp.arange(n_freq, dtype=F32) / n_freq)
    ang_r = row_pos[:, None] * inv_freq[None, :]
    ang_c = col_pos[:, None] * inv_freq[None, :]
    ones = jnp.ones((DEC_SEQ, LANES - ROPE_DIM), F32)
    zeros = jnp.zeros((DEC_SEQ, LANES - ROPE_DIM), F32)
    cos = jnp.concatenate([jnp.cos(ang_r), jnp.cos(ang_r), jnp.cos(ang_c), jnp.cos(ang_c), ones], axis=-1)
    sin = jnp.concatenate([-jnp.sin(ang_r), jnp.sin(ang_r), -jnp.sin(ang_c), jnp.sin(ang_c), zeros], axis=-1)
    cos = jnp.concatenate([jnp.ones((TM, LANES), F32), cos], axis=0)
    sin = jnp.concatenate([jnp.zeros((TM, LANES), F32), sin], axis=0)
    return cos, sin


def _pool_band():
    r = np.arange(TM)[:, None]
    c = np.arange(TM + 2 * HALO)[None, :] - HALO
    bands = [((c >= r - w // 2) & (c < r + w // 2)).astype(np.float32) for w in POOL_WINDOWS]
    return jnp.asarray(np.stack(bands), dtype=BF16)


def _strict_lower():
    r = np.arange(TM)
    return jnp.asarray((r[None, :] < r[:, None]).astype(np.float32), dtype=BF16)


def _layer_weights(l, p):
    o1 = Q_LORA
    o2 = o1 + KV_LORA
    o3 = o2 + ROPE_DIM
    w_in = p["w_in"][l]
    w_in_p = jnp.concatenate(
        [w_in[:, :o3], jnp.zeros((D_MODEL, LANES - ROPE_DIM), F32), w_in[:, o3:]], axis=1).astype(BF16)
    w_q = p["w_q_up"][l].reshape(Q_LORA, N_HEADS, QK_DIM)
    w_q = jnp.pad(w_q, ((0, 0), (0, 0), (0, HEAD_PAD - QK_DIM))).reshape(Q_LORA, QK_WIDTH).astype(BF16)
    w_kv = p["w_kv_up"][l].reshape(KV_LORA, N_HEADS, NOPE_DIM + V_DIM)
    w_kv = jnp.concatenate([w_kv[:, :, :NOPE_DIM].reshape(KV_LORA, -1),
                            w_kv[:, :, NOPE_DIM:].reshape(KV_LORA, -1)], axis=1).astype(BF16)
    pad_h = (0, HEAD_PAD - QK_DIM)
    w_router = jnp.concatenate([p["w_group_router"][l], p["w_expert_router"][l]], axis=1)
    w_router = jnp.pad(w_router, ((0, 0), (0, LANES - w_router.shape[1])))
    w_router_hi = w_router.astype(BF16)
    b_router = jnp.concatenate([p["b_group_router"][l], p["b_expert_router"][l]])
    return {
        "norm_mix": p["norm_mix"][l][None, :],
        "norm_ffn": p["norm_ffn"][l][None, :],
        "w_in": w_in_p,
        "q_lora_norm": p["q_lora_norm"][l][None, :],
        "w_q": w_q,
        "kv_lora_norm": p["kv_lora_norm"][l][None, :],
        "w_kv": w_kv,
        "q_head_norm": jnp.pad(p["q_head_norm"][l] * ATTN_SCALE, pad_h)[None, :],
        "k_head_norm": jnp.pad(p["k_head_norm"][l], pad_h)[None, :],
        "w_pool": p["w_pool"][l].astype(BF16),
        "pool_scale": p["pool_scale"][l][None, :],
        "attn_out_norm": p["attn_out_norm"][l][None, :],
        "pool_out_norm": p["pool_out_norm"][l][None, :],
        "w_out": p["w_out"][l].astype(BF16),
        "w_router_hi": w_router_hi,
        "w_router_lo": (w_router - w_router_hi.astype(F32)).astype(BF16),
        "b_router": jnp.pad(b_router, (0, LANES - b_router.shape[0]))[None, :],
    }


def kernel(x_prompt, x_sample, cache_ckv, cache_krope, c, c_ctx, w_ada, b_ada, norm_mix, norm_ffn, w_in,
           q_lora_norm, w_q_up, kv_lora_norm, w_kv_up, q_head_norm, k_head_norm, w_pool, pool_scale,
           attn_out_norm, pool_out_norm, w_out, w_group_router, b_group_router, w_expert_router,
           b_expert_router, w_gate, w_up, w_down):
    p = dict(norm_mix=norm_mix, norm_ffn=norm_ffn, w_in=w_in, q_lora_norm=q_lora_norm, w_q_up=w_q_up,
             kv_lora_norm=kv_lora_norm, w_kv_up=w_kv_up, q_head_norm=q_head_norm, k_head_norm=k_head_norm,
             w_pool=w_pool, pool_scale=pool_scale, attn_out_norm=attn_out_norm, pool_out_norm=pool_out_norm,
             w_out=w_out, w_group_router=w_group_router, b_group_router=b_group_router,
             w_expert_router=w_expert_router, b_expert_router=b_expert_router)

    cond8 = jnp.concatenate([c_ctx[None, :], c, jnp.zeros((8 - 1 - DEC_BATCH, D_MODEL), F32)], axis=0)
    mod = _ada_mod(cond8, w_ada, b_ada).reshape(DEPTH * 8, 6, D_MODEL)
    rope_cos, rope_sin = _rope_tables()
    band = _pool_band()
    tri = _strict_lower()

    x = jnp.concatenate([x_prompt.reshape(N_CTX_TOK, D_MODEL), x_sample.reshape(N_S_TOK, D_MODEL)], axis=0)
    ckv_out = []
    krope_out = []
    for l in range(DEPTH):
        lw = _layer_weights(l, p)
        q, k, v, ckv, krope, u = _in_proj(x, mod, lw, rope_cos, rope_sin, l)
        ckv_out.append(ckv[:N_CTX_TOK].reshape(BATCH, SEQ, KV_LORA))
        krope_out.append(krope[:N_CTX_TOK].reshape(BATCH, SEQ, ROPE_DIM))

        cache_kr = jnp.pad(cache_krope[:, l].reshape(DEC_BATCH * PAST_LEN, ROPE_DIM),
                           ((0, 0), (0, LANES - ROPE_DIM)))
        kc, vc = _cache_keys(cache_ckv[:, l].reshape(DEC_BATCH * PAST_LEN, KV_LORA), cache_kr, lw)
        attn = _attention(q, k, v, kc, vc)

        x1, h2p, slab, cnt = _mix(x, attn, u, mod, lw, band, tri, l)

        counts = cnt[0, :N_EXPERTS].astype(jnp.int32)
        tile_i, exp_i, lo_i, hi_i, flag_i, seg_off = _work_items(counts)
        eids = slab[:, SLAB_E0:SLAB_E1 + 1].astype(jnp.int32)
        ranks = slab[:, SLAB_R0:SLAB_R1 + 1].astype(jnp.int32)
        onehot = (eids[:, :, None] == jnp.arange(N_EXPERTS, dtype=jnp.int32)[None, None, :])
        pos = jnp.sum(jnp.where(onehot, seg_off[None, None, :], 0), axis=-1) + ranks
        pos = pos.reshape(N_TILES, 1, 2 * TM)

        xs = _dispatch(h2p, pos)
        ys = _moe((tile_i, exp_i, lo_i, hi_i, flag_i), xs, w_gate, w_up, w_down, l)
        x = _combine(pos, x1, slab, mod, ys, l)

    y_p = x[:N_CTX_TOK].reshape(BATCH, SEQ, D_MODEL)
    y_s = x[N_CTX_TOK:].reshape(DEC_BATCH, DEC_SEQ, D_MODEL)
    new_ckv = jnp.stack(ckv_out, axis=1)
    new_krope = jnp.stack(krope_out, axis=1)
    return (y_p, y_s, new_ckv, new_krope)
```
